```python
import jax, jax.numpy as jnp
from jax import lax
import numpy as np

D_MODEL = 2048
BATCH = 8
SEQ = 4096
DEPTH = 1
DEC_BATCH = 4
DEC_SEQ = 2048
PAST_LEN = 128

N_HEADS = 8
HEAD_DIM = 64
V_DIM = 2 * HEAD_DIM
Q_COLS = N_HEADS * 2 * HEAD_DIM
ATTN_WIDTH = N_HEADS * V_DIM
ROPE_THETA = 10000.0
Q_BLOCK = 128
CONV_WIDTH = D_MODEL // 2
CONV_K = 3
GATE_COLS = 2 * D_MODEL
IN_COLS = 2 * Q_COLS + ATTN_WIDTH + 3 * CONV_WIDTH + GATE_COLS
SPLITS = (Q_COLS, 2 * Q_COLS, 2 * Q_COLS + ATTN_WIDTH,
          2 * Q_COLS + ATTN_WIDTH + CONV_WIDTH,
          2 * Q_COLS + ATTN_WIDTH + 2 * CONV_WIDTH,
          2 * Q_COLS + ATTN_WIDTH + 3 * CONV_WIDTH)
N_EXPERTS = 32
TOP_K = 4
D_FF = D_MODEL
SWIGLU_LIMIT = 7.0
SWIGLU_ALPHA = 1.702
MOE_BLOCK = 256
LN_EPS = 1e-5
RMS_EPS = 1e-5
DN_ALPHA = (2.0 * DEPTH) ** 0.25
DN_BETA = (8.0 * DEPTH) ** -0.25

kernel_name = "hybrid_diffattn_shortconv_moe_encoder"


def layer_norm(x, g, b):
    xf = x.astype(jnp.float32)
    mu = jnp.mean(xf, -1, keepdims=True)
    var = jnp.mean(jnp.square(xf - mu), -1, keepdims=True)
    return ((xf - mu) * lax.rsqrt(var + LN_EPS) * g + b).astype(x.dtype)


def rope_tables(seq):
    inv = ROPE_THETA ** (-jnp.arange(0, HEAD_DIM, 2, dtype=jnp.float32) / HEAD_DIM)
    ang = jnp.arange(seq, dtype=jnp.float32)[:, None] * inv[None, :]
    ang = jnp.concatenate([ang, ang], -1)
    return jnp.cos(ang), jnp.sin(ang)


def apply_rope(t, cos, sin):
    t1, t2 = jnp.split(t, 2, axis=-1)
    rot = jnp.concatenate([-t2, t1], -1)
    c = cos[None, :, None, None, :]
    s = sin[None, :, None, None, :]
    return (t * c + rot * s).astype(t.dtype)


def diff_attention(q, k, v, lam):
    B, S = q.shape[0], q.shape[1]
    nb = S // Q_BLOCK
    qb = q.reshape(B, nb, Q_BLOCK, N_HEADS, 2, HEAD_DIM).transpose(1, 0, 2, 3, 4, 5)

    def one_block(qblk):
        s = jnp.einsum('bqhcd,bkhcd->bhcqk', qblk, k, preferred_element_type=jnp.float32)
        p = jax.nn.softmax(s, axis=-1)
        pd = p[:, :, 0] - lam * p[:, :, 1]
        return jnp.einsum('bhqk,bkhe->bqhe', pd.astype(v.dtype), v)

    o = lax.map(one_block, qb)
    return o.transpose(1, 0, 2, 3, 4).reshape(B, S, N_HEADS, V_DIM)


def short_conv(bg, cg, xc, w):
    h = cg * xc
    S = h.shape[1]
    hp = jnp.pad(h, ((0, 0), (1, 1), (0, 0)))
    conv = hp[:, :S] * w[0] + hp[:, 1:S + 1] * w[1] + hp[:, 2:] * w[2]
    return bg * conv


def moe(x, w_router, b_router, w1, b1, w2, b2):
    B, S, D = x.shape
    T = B * S
    xf = x.reshape(T, D)
    logits = (xf @ w_router).astype(jnp.float32) + b_router
    top_v, top_i = lax.top_k(logits, TOP_K)
    gates = jax.nn.softmax(top_v, axis=-1).astype(x.dtype)
    n = T * TOP_K
    flat_e = top_i.reshape(n)
    order = jnp.argsort(flat_e)
    sorted_e = flat_e[order]
    counts = jnp.bincount(flat_e, length=N_EXPERTS)
    padded = (counts + MOE_BLOCK - 1) // MOE_BLOCK * MOE_BLOCK
    starts = jnp.cumsum(counts) - counts
    pends = jnp.cumsum(padded)
    pstarts = pends - padded
    dest_sorted = (pstarts[sorted_e] + jnp.arange(n) - starts[sorted_e]).astype(jnp.int32)
    dest = jnp.zeros((n,), jnp.int32).at[order].set(dest_sorted)
    n_rows = n + N_EXPERTS * MOE_BLOCK
    n_blk = n_rows // MOE_BLOCK
    row_tok = jnp.full((n_rows,), T, jnp.int32).at[dest].set(jnp.arange(n, dtype=jnp.int32) // TOP_K)
    x_pad = jnp.concatenate([xf, jnp.zeros((1, D), xf.dtype)], 0)
    xs = x_pad[row_tok].reshape(n_blk, MOE_BLOCK, D)
    blk_e = jnp.minimum(jnp.searchsorted(pends, jnp.arange(n_blk) * MOE_BLOCK, side='right'),
                        N_EXPERTS - 1)

    def expert_block(args):
        xb, e = args
        h = xb @ w1[e] + b1[e]
        gate = jnp.minimum(h[:, :D_FF], SWIGLU_LIMIT)
        up = jnp.clip(h[:, D_FF:], -SWIGLU_LIMIT, SWIGLU_LIMIT)
        act = gate * jax.nn.sigmoid(SWIGLU_ALPHA * gate)
        return ((up + 1.0) * act) @ w2[e] + b2[e]

    ys = lax.map(expert_block, (xs, blk_e)).reshape(n_rows, D)
    y = jnp.sum(ys[dest].reshape(T, TOP_K, D) * gates[..., None], axis=1)
    return y.reshape(B, S, D)


def encoder_layer(x, l, w_in, b_gate, lambda_q1, lambda_k1, lambda_q2, lambda_k2, subln_g,
                  conv_w, w_attn_out, w_conv_out, w_o, ln1_g, ln1_b,
                  w_router, b_router, w1, b1, w2, b2, ln2_g, ln2_b):
    B, S, _ = x.shape
    lambda_init = 0.8 - 0.6 * float(np.exp(-0.3 * l))
    proj = x @ w_in
    q, k, v, cb, cc, cx, g = jnp.split(proj, SPLITS, axis=-1)
    cos, sin = rope_tables(S)
    q = apply_rope(q.reshape(B, S, N_HEADS, 2, HEAD_DIM), cos, sin) * (HEAD_DIM ** -0.5)
    k = apply_rope(k.reshape(B, S, N_HEADS, 2, HEAD_DIM), cos, sin)
    v = v.reshape(B, S, N_HEADS, V_DIM)
    lam = (jnp.exp(jnp.sum(lambda_q1.astype(jnp.float32) * lambda_k1.astype(jnp.float32)))
           - jnp.exp(jnp.sum(lambda_q2.astype(jnp.float32) * lambda_k2.astype(jnp.float32)))
           + lambda_init)
    o = diff_attention(q, k, v, lam).astype(jnp.float32)
    o = o * lax.rsqrt(jnp.mean(o * o, -1, keepdims=True) + RMS_EPS) * subln_g * (1.0 - lambda_init)
    branch_a = o.astype(x.dtype).reshape(B, S, ATTN_WIDTH) @ w_attn_out
    branch_c = short_conv(cb, cc, cx, conv_w) @ w_conv_out
    gate = jax.nn.sigmoid(g + b_gate)
    mix = (gate[..., :D_MODEL] * branch_a + gate[..., D_MODEL:] * branch_c) @ w_o
    x = layer_norm(DN_ALPHA * x + mix, ln1_g, ln1_b)
    x = layer_norm(DN_ALPHA * x + moe(x, w_router, b_router, w1, b1, w2, b2), ln2_g, ln2_b)
    return x


def trunk(x, params):
    for l in range(DEPTH):
        x = encoder_layer(x, l, *[p[l] for p in params])
    return x


def setup_inputs(seed: int = 0) -> dict:
    key = jax.random.key(seed)
    ks = jax.random.split(key, 32)
    nrm = lambda k, shape, scale: jax.random.normal(k, shape, jnp.float32) * scale
    sd = D_MODEL ** -0.5
    w_in = jnp.concatenate([
        nrm(ks[2], (DEPTH, D_MODEL, 2 * Q_COLS), sd),
        nrm(ks[3], (DEPTH, D_MODEL, ATTN_WIDTH), sd * DN_BETA),
        nrm(ks[4], (DEPTH, D_MODEL, 2 * CONV_WIDTH), sd),
        nrm(ks[5], (DEPTH, D_MODEL, CONV_WIDTH), sd * DN_BETA),
        nrm(ks[6], (DEPTH, D_MODEL, GATE_COLS), sd),
    ], axis=-1)
    return {
        "x_prompt": jax.random.normal(ks[0], (BATCH, SEQ, D_MODEL), jnp.float32),
        "x_sample": jax.random.normal(ks[1], (DEC_BATCH, DEC_SEQ, D_MODEL), jnp.float32),
        "w_in": w_in,
        "b_gate": nrm(ks[7], (DEPTH, GATE_COLS), 0.01),
        "lambda_q1": nrm(ks[8], (DEPTH, HEAD_DIM), 0.1),
        "lambda_k1": nrm(ks[9], (DEPTH, HEAD_DIM), 0.1),
        "lambda_q2": nrm(ks[10], (DEPTH, HEAD_DIM), 0.1),
        "lambda_k2": nrm(ks[11], (DEPTH, HEAD_DIM), 0.1),
        "subln_g": 1.0 + nrm(ks[12], (DEPTH, V_DIM), 0.01),
        "conv_w": nrm(ks[13], (DEPTH, CONV_K, CONV_WIDTH), CONV_K ** -0.5),
        "w_attn_out": nrm(ks[14], (DEPTH, ATTN_WIDTH, D_MODEL), ATTN_WIDTH ** -0.5 * DN_BETA),
        "w_conv_out": nrm(ks[15], (DEPTH, CONV_WIDTH, D_MODEL), CONV_WIDTH ** -0.5 * DN_BETA),
        "w_o": nrm(ks[16], (DEPTH, D_MODEL, D_MODEL), sd * DN_BETA),
        "ln1_g": 1.0 + nrm(ks[17], (DEPTH, D_MODEL), 0.01),
        "ln1_b": nrm(ks[18], (DEPTH, D_MODEL), 0.01),
        "w_router": nrm(ks[19], (DEPTH, D_MODEL, N_EXPERTS), sd),
        "b_router": nrm(ks[20], (DEPTH, N_EXPERTS), 0.01),
        "w1": nrm(ks[21], (DEPTH, N_EXPERTS, D_MODEL, 2 * D_FF), sd),
        "b1": nrm(ks[22], (DEPTH, N_EXPERTS, 2 * D_FF), 0.01),
        "w2": nrm(ks[23], (DEPTH, N_EXPERTS, D_FF, D_MODEL), D_FF ** -0.5 * DN_BETA),
        "b2": nrm(ks[24], (DEPTH, N_EXPERTS, D_MODEL), 0.01),
        "ln2_g": 1.0 + nrm(ks[25], (DEPTH, D_MODEL), 0.01),
        "ln2_b": nrm(ks[26], (DEPTH, D_MODEL), 0.01),
    }


def reference(x_prompt, x_sample, w_in, b_gate, lambda_q1, lambda_k1, lambda_q2, lambda_k2,
              subln_g, conv_w, w_attn_out, w_conv_out, w_o, ln1_g, ln1_b,
              w_router, b_router, w1, b1, w2, b2, ln2_g, ln2_b):
    params = (w_in, b_gate, lambda_q1, lambda_k1, lambda_q2, lambda_k2, subln_g, conv_w,
              w_attn_out, w_conv_out, w_o, ln1_g, ln1_b,
              w_router, b_router, w1, b1, w2, b2, ln2_g, ln2_b)
    y_prompt = trunk(x_prompt, params)
    y_sample = trunk(x_sample, params)
    return (y_prompt, y_sample)
```

```python
import functools

import numpy as np
import jax
import jax.numpy as jnp
from jax import lax
from jax.experimental import pallas as pl
from jax.experimental.pallas import tpu as pltpu

D_MODEL = 2048
N_HEADS = 8
HEAD_DIM = 64
V_DIM = 2 * HEAD_DIM
Q_COLS = N_HEADS * 2 * HEAD_DIM
ATTN_WIDTH = N_HEADS * V_DIM
ROPE_THETA = 10000.0
CONV_WIDTH = D_MODEL // 2
GATE_COLS = 2 * D_MODEL
IN_COLS = 2 * Q_COLS + ATTN_WIDTH + 3 * CONV_WIDTH + GATE_COLS
N_EXPERTS = 32
TOP_K = 4
D_FF = D_MODEL
SWIGLU_LIMIT = 7.0
SWIGLU_ALPHA = 1.702
LN_EPS = 1e-5
RMS_EPS = 1e-5
DEPTH = 1
DN_ALPHA = (2.0 * DEPTH) ** 0.25
LAMBDA_INIT = 0.8 - 0.6 * float(np.exp(-0.3 * 0))

LANES = 128
BF16_SUBLANES = 16
VMEM_LIMIT = 56 * 1024 * 1024

PROJ_TN = 1024
COL_Q, COL_K, COL_V, COL_CB, COL_CC, COL_CX = 0, 1, 2, 3, 4, 5
COL_GATE0 = 6
N_PROJ_TILES = IN_COLS // PROJ_TN

TM_PROJ = 512
TQ = 256
TM_POST = 256
TG = 256
TM_MOE = 1024
SUB_MOE = 256
TF_MOE = 512
N_F = D_FF // TF_MOE
TC = 128

f32 = jnp.float32
bf16 = jnp.bfloat16


def _cparams(n_axes):
    return pltpu.CompilerParams(dimension_semantics=("arbitrary",) * n_axes,
                                vmem_limit_bytes=VMEM_LIMIT)


def _in_proj_kernel(x_ref, w_ref, cos_ref, sin_ref, bias_ref, o_ref, xb_ref):
    j = pl.program_id(1)

    @pl.when(j == 0)
    def _():
        xb_ref[...] = x_ref[...].astype(bf16)

    acc = jnp.dot(xb_ref[...], w_ref[...], preferred_element_type=f32)

    @pl.when(j <= COL_K)
    def _():
        scale = jnp.where(j == COL_Q, HEAD_DIM ** -0.5, 1.0).astype(f32)
        cos = cos_ref[...]
        sin = sin_ref[...]
        first_half = (lax.broadcasted_iota(jnp.int32, cos.shape, 1) % HEAD_DIM) < HEAD_DIM // 2
        for c in range(PROJ_TN // LANES):
            t = acc[:, c * LANES:(c + 1) * LANES]
            rot = jnp.where(first_half,
                            pltpu.roll(t, LANES - HEAD_DIM // 2, 1),
                            pltpu.roll(t, HEAD_DIM // 2, 1))
            o_ref[:, c * LANES:(c + 1) * LANES] = ((t * cos + rot * sin) * scale).astype(o_ref.dtype)

    @pl.when((j > COL_K) & (j < COL_GATE0))
    def _():
        o_ref[...] = acc.astype(o_ref.dtype)

    @pl.when(j >= COL_GATE0)
    def _():
        o_ref[...] = jax.nn.sigmoid(acc + bias_ref[...]).astype(o_ref.dtype)


def _in_proj(x2d, w_in_b, cos, sin_signed, bias_full, seq):
    T = x2d.shape[0]
    tm = TM_PROJ
    pos_tiles = seq // tm
    return pl.pallas_call(
        _in_proj_kernel,
        grid=(T // tm, N_PROJ_TILES),
        in_specs=[
            pl.BlockSpec((tm, D_MODEL), lambda i, j: (i, 0)),
            pl.BlockSpec((D_MODEL, PROJ_TN), lambda i, j: (0, j)),
            pl.BlockSpec((tm, LANES), lambda i, j: (i % pos_tiles, 0)),
            pl.BlockSpec((tm, LANES), lambda i, j: (i % pos_tiles, 0)),
            pl.BlockSpec((1, PROJ_TN), lambda i, j: (0, j)),
        ],
        out_specs=pl.BlockSpec((tm, PROJ_TN), lambda i, j: (i, j)),
        out_shape=jax.ShapeDtypeStruct((T, IN_COLS), bf16),
        scratch_shapes=[pltpu.VMEM((tm, D_MODEL), bf16)],
        compiler_params=_cparams(2),
        name="in_proj",
    )(x2d, w_in_b, cos, sin_signed, bias_full)


def _attn_kernel(lam_ref, q_ref, k_ref, v_ref, g_ref, o_ref, kt_ref):
    i = pl.program_id(2)

    @pl.when(i == 0)
    def _():
        kt_ref[...] = k_ref[...].astype(f32).T.astype(bf16)

    lam = lam_ref[0, 0]
    q = q_ref[...]
    first = lax.broadcasted_iota(jnp.int32, q.shape, 1) < HEAD_DIM
    zero = jnp.zeros_like(q)
    kt = kt_ref[...]
    s1 = jnp.dot(jnp.where(first, q, zero), kt, preferred_element_type=f32)
    s2 = jnp.dot(jnp.where(first, zero, q), kt, preferred_element_type=f32)
    e1 = jnp.exp(s1 - jnp.max(s1, axis=-1, keepdims=True))
    e2 = jnp.exp(s2 - jnp.max(s2, axis=-1, keepdims=True))
    a1 = 1.0 / jnp.sum(e1, axis=-1, keepdims=True)
    a2 = lam / jnp.sum(e2, axis=-1, keepdims=True)
    pd = (e1 * a1 - e2 * a2).astype(bf16)
    o = jnp.dot(pd, v_ref[...], preferred_element_type=f32)
    o = o * lax.rsqrt(jnp.mean(o * o, axis=-1, keepdims=True) + RMS_EPS)
    o_ref[...] = (o * g_ref[...] * (1.0 - LAMBDA_INIT)).astype(o_ref.dtype)


def _attention(proj3, lam, subln_g):
    B, S, _ = proj3.shape
    hb = PROJ_TN // LANES
    return pl.pallas_call(
        _attn_kernel,
        grid=(B, N_HEADS, S // TQ),
        in_specs=[
            pl.BlockSpec(memory_space=pltpu.SMEM),
            pl.BlockSpec((None, TQ, LANES), lambda b, h, i: (b, i, COL_Q * hb + h)),
            pl.BlockSpec((None, S, LANES), lambda b, h, i: (b, 0, COL_K * hb + h)),
            pl.BlockSpec((None, S, LANES), lambda b, h, i: (b, 0, COL_V * hb + h)),
            pl.BlockSpec((1, LANES), lambda b, h, i: (0, 0)),
        ],
        out_specs=pl.BlockSpec((None, TQ, LANES), lambda b, h, i: (b, i, h)),
        out_shape=jax.ShapeDtypeStruct((B, S, ATTN_WIDTH), bf16),
        scratch_shapes=[pltpu.VMEM((LANES, S), bf16)],
        compiler_params=_cparams(3),
        name="diff_attn",
    )(lam, proj3, proj3, proj3, subln_g)


def _layer_norm(r, g, b):
    mu = jnp.mean(r, axis=-1, keepdims=True)
    d = r - mu
    var = jnp.mean(d * d, axis=-1, keepdims=True)
    return d * lax.rsqrt(var + LN_EPS) * g + b


def _post_kernel(seq, o_ref, cb_ref, cc_ref, cx_ref, ccp_ref, cxp_ref, ccn_ref, cxn_ref,
                 ga_ref, gc_ref, x_ref, wa_ref, wc_ref, wo_ref, cw_ref, g1_ref, b1_ref,
                 wrh_ref, wrl_ref, br_ref, x1_ref, lg_ref):
    i = pl.program_id(0)
    tm = x_ref.shape[0]
    h = cc_ref[...].astype(f32) * cx_ref[...].astype(f32)
    last = BF16_SUBLANES - 1
    h_prev = ccp_ref[last:last + 1, :].astype(f32) * cxp_ref[last:last + 1, :].astype(f32)
    h_next = ccn_ref[0:1, :].astype(f32) * cxn_ref[0:1, :].astype(f32)
    h_prev = jnp.where((i * tm) % seq == 0, 0.0, h_prev)
    h_next = jnp.where(((i + 1) * tm) % seq == 0, 0.0, h_next)
    row = lax.broadcasted_iota(jnp.int32, h.shape, 0)
    h_m1 = jnp.where(row == 0, h_prev, pltpu.roll(h, 1, 0))
    h_p1 = jnp.where(row == tm - 1, h_next, pltpu.roll(h, tm - 1, 0))
    cw = cw_ref[...]
    conv = h_m1 * cw[0:1, :] + h * cw[1:2, :] + h_p1 * cw[2:3, :]
    yc = (cb_ref[...].astype(f32) * conv).astype(bf16)
    branch_a = jnp.dot(o_ref[...], wa_ref[...], preferred_element_type=f32)
    branch_c = jnp.dot(yc, wc_ref[...], preferred_element_type=f32)
    mix_in = ga_ref[...].astype(f32) * branch_a + gc_ref[...].astype(f32) * branch_c
    mix = jnp.dot(mix_in.astype(bf16), wo_ref[...], preferred_element_type=f32)
    x1 = _layer_norm(DN_ALPHA * x_ref[...] + mix, g1_ref[...], b1_ref[...])
    x1_ref[...] = x1
    hi = x1.astype(bf16)
    lo = (x1 - hi.astype(f32)).astype(bf16)
    lg = (jnp.dot(hi, wrh_ref[...], preferred_element_type=f32)
          + jnp.dot(lo, wrh_ref[...], preferred_element_type=f32)
          + jnp.dot(hi, wrl_ref[...], preferred_element_type=f32))
    lg_ref[...] = lg + br_ref[...]


def _post(o2d, proj, x2d, wa, wc, wo, conv_w, g1, b1, wr_hi, wr_lo, br, seq):
    T = x2d.shape[0]
    tm = TM_POST
    halo = tm // BF16_SUBLANES
    n_halo = T // BF16_SUBLANES
    gate_blk = COL_GATE0 * PROJ_TN // D_MODEL

    def const(shape):
        return pl.BlockSpec(shape, lambda i: (0,) * len(shape))

    def prev_spec(col):
        return pl.BlockSpec((BF16_SUBLANES, PROJ_TN), lambda i: (jnp.maximum(i * halo - 1, 0), col))

    def next_spec(col):
        return pl.BlockSpec((BF16_SUBLANES, PROJ_TN),
                            lambda i: (jnp.minimum((i + 1) * halo, n_halo - 1), col))

    return pl.pallas_call(
        functools.partial(_post_kernel, seq),
        grid=(T // tm,),
        in_specs=[
            pl.BlockSpec((tm, ATTN_WIDTH), lambda i: (i, 0)),
            pl.BlockSpec((tm, PROJ_TN), lambda i: (i, COL_CB)),
            pl.BlockSpec((tm, PROJ_TN), lambda i: (i, COL_CC)),
            pl.BlockSpec((tm, PROJ_TN), lambda i: (i, COL_CX)),
            prev_spec(COL_CC), prev_spec(COL_CX), next_spec(COL_CC), next_spec(COL_CX),
            pl.BlockSpec((tm, D_MODEL), lambda i: (i, gate_blk)),
            pl.BlockSpec((tm, D_MODEL), lambda i: (i, gate_blk + 1)),
            pl.BlockSpec((tm, D_MODEL), lambda i: (i, 0)),
            const((ATTN_WIDTH, D_MODEL)), const((CONV_WIDTH, D_MODEL)), const((D_MODEL, D_MODEL)),
            const((3, CONV_WIDTH)), const((1, D_MODEL)), const((1, D_MODEL)),
            const((D_MODEL, N_EXPERTS)), const((D_MODEL, N_EXPERTS)), const((1, N_EXPERTS)),
        ],
        out_specs=[pl.BlockSpec((tm, D_MODEL), lambda i: (i, 0)),
                   pl.BlockSpec((tm, N_EXPERTS), lambda i: (i, 0))],
        out_shape=[jax.ShapeDtypeStruct((T, D_MODEL), f32),
                   jax.ShapeDtypeStruct((T, N_EXPERTS), f32)],
        compiler_params=_cparams(1),
        name="post_attn",
    )(o2d, proj, proj, proj, proj, proj, proj, proj, proj, proj, x2d,
      wa, wc, wo, conv_w, g1, b1, wr_hi, wr_lo, br)


def _gather_kernel(idx_ref, src_ref, o_ref, buf_ref, sem):
    def issue(r, carry):
        pltpu.make_async_copy(src_ref.at[pl.ds(idx_ref[r], 1), :],
                              buf_ref.at[pl.ds(r, 1), :], sem).start()
        return carry

    lax.fori_loop(0, TG, issue, 0)
    pltpu.make_async_copy(src_ref.at[pl.ds(0, TG), :], buf_ref, sem).wait()
    o_ref[...] = buf_ref[...].astype(o_ref.dtype)


def _gather_rows(src, idx):
    n_rows = idx.shape[0]
    width = src.shape[1]
    return pl.pallas_call(
        _gather_kernel,
        grid=(n_rows // TG,),
        in_specs=[
            pl.BlockSpec((TG,), lambda i: (i,), memory_space=pltpu.SMEM),
            pl.BlockSpec(memory_space=pl.ANY),
        ],
        out_specs=pl.BlockSpec((TG, width), lambda i: (i, 0)),
        out_shape=jax.ShapeDtypeStruct((n_rows, width), bf16),
        scratch_shapes=[pltpu.VMEM((TG, width), src.dtype), pltpu.SemaphoreType.DMA(())],
        compiler_params=_cparams(1),
        name="moe_gather",
    )(idx, src)


def _moe_kernel(te_ref, ns_ref, nu_ref, x_ref, w1g_ref, w1u_ref, b1g_ref, b1u_ref,
                w2_ref, b2_ref, o_ref):
    t = pl.program_id(0)
    c = pl.program_id(1)

    def body(s, carry):
        r0 = pl.multiple_of(s * SUB_MOE, SUB_MOE)
        xb = x_ref[pl.ds(r0, SUB_MOE), :]
        hg = jnp.dot(xb, w1g_ref[...], preferred_element_type=f32) + b1g_ref[...]
        hu = jnp.dot(xb, w1u_ref[...], preferred_element_type=f32) + b1u_ref[...]
        gate = jnp.minimum(hg, SWIGLU_LIMIT)
        up = jnp.clip(hu, -SWIGLU_LIMIT, SWIGLU_LIMIT)
        act = gate * jax.nn.sigmoid(SWIGLU_ALPHA * gate)
        a = ((up + 1.0) * act).astype(bf16)
        y = jnp.dot(a, w2_ref[...], preferred_element_type=f32)

        @pl.when(c == 0)
        def _():
            o_ref[pl.ds(r0, SUB_MOE), :] = y + b2_ref[...]

        @pl.when(c > 0)
        def _():
            o_ref[pl.ds(r0, SUB_MOE), :] += y

        return carry

    lax.fori_loop(0, ns_ref[t], body, 0)

    @pl.when(c == 0)
    def _():
        def clear(s, carry):
            r0 = pl.multiple_of(s * SUB_MOE, SUB_MOE)
            o_ref[pl.ds(r0, SUB_MOE), :] = jnp.zeros((SUB_MOE, D_MODEL), f32)
            return carry

        lax.fori_loop(ns_ref[t], TM_MOE // SUB_MOE, clear, 0)


def _moe_experts(xs, w1b, b1, w2b, b2, tile_expert, tile_nsub, n_used):
    n_rows = xs.shape[0]
    n_tiles = n_rows // TM_MOE

    def used_tile(t, nu):
        return jnp.minimum(t, nu[0] - 1)

    def f_idx(t, c, nu):
        return jnp.where(t < nu[0], c, N_F - 1)

    grid_spec = pltpu.PrefetchScalarGridSpec(
        num_scalar_prefetch=3,
        grid=(n_tiles, N_F),
        in_specs=[
            pl.BlockSpec((TM_MOE, D_MODEL), lambda t, c, te, ns, nu: (used_tile(t, nu), 0)),
            pl.BlockSpec((None, D_MODEL, TF_MOE), lambda t, c, te, ns, nu: (te[t], 0, f_idx(t, c, nu))),
            pl.BlockSpec((None, D_MODEL, TF_MOE),
                         lambda t, c, te, ns, nu: (te[t], 0, N_F + f_idx(t, c, nu))),
            pl.BlockSpec((None, 1, TF_MOE), lambda t, c, te, ns, nu: (te[t], 0, f_idx(t, c, nu))),
            pl.BlockSpec((None, 1, TF_MOE), lambda t, c, te, ns, nu: (te[t], 0, N_F + f_idx(t, c, nu))),
            pl.BlockSpec((None, TF_MOE, D_MODEL), lambda t, c, te, ns, nu: (te[t], f_idx(t, c, nu), 0)),
            pl.BlockSpec((None, 1, D_MODEL), lambda t, c, te, ns, nu: (te[t], 0, 0)),
        ],
        out_specs=pl.BlockSpec((TM_MOE, D_MODEL), lambda t, c, te, ns, nu: (t, 0)),
    )
    return pl.pallas_call(
        _moe_kernel,
        grid_spec=grid_spec,
        out_shape=jax.ShapeDtypeStruct((n_rows, D_MODEL), f32),
        compiler_params=_cparams(2),
        name="moe_experts",
    )(tile_expert, tile_nsub, n_used, xs, w1b, w1b, b1, b1, w2b, b2)


def _combine_kernel(dest_ref, ys_ref, x1_ref, gates_ref, g_ref, b_ref, o_ref, buf_ref, sem):
    def issue(r, carry):
        for k in range(TOP_K):
            pltpu.make_async_copy(ys_ref.at[pl.ds(dest_ref[r * TOP_K + k], 1), :],
                                  buf_ref.at[k, pl.ds(r, 1), :], sem).start()
        return carry

    lax.fori_loop(0, TC, issue, 0)
    for k in range(TOP_K):
        pltpu.make_async_copy(ys_ref.at[pl.ds(0, TC), :], buf_ref.at[k], sem).wait()
    gates = gates_ref[...]
    y = gates[:, 0:1] * buf_ref[0]
    for k in range(1, TOP_K):
        y = y + gates[:, k:k + 1] * buf_ref[k]
    o_ref[...] = _layer_norm(DN_ALPHA * x1_ref[...] + y, g_ref[...], b_ref[...])


def _combine(dest, ys, x1, gates, g2, b2):
    T = x1.shape[0]
    return pl.pallas_call(
        _combine_kernel,
        grid=(T // TC,),
        in_specs=[
            pl.BlockSpec((TC * TOP_K,), lambda i: (i,), memory_space=pltpu.SMEM),
            pl.BlockSpec(memory_space=pl.ANY),
            pl.BlockSpec((TC, D_MODEL), lambda i: (i, 0)),
            pl.BlockSpec((TC, TOP_K), lambda i: (i, 0)),
            pl.BlockSpec((1, D_MODEL), lambda i: (0, 0)),
            pl.BlockSpec((1, D_MODEL), lambda i: (0, 0)),
        ],
        out_specs=pl.BlockSpec((TC, D_MODEL), lambda i: (i, 0)),
        out_shape=jax.ShapeDtypeStruct((T, D_MODEL), f32),
        scratch_shapes=[pltpu.VMEM((TOP_K, TC, D_MODEL), f32), pltpu.SemaphoreType.DMA(())],
        compiler_params=_cparams(1),
        name="moe_combine",
    )(dest, ys, x1, gates, g2, b2)


def _route(logits):
    T = logits.shape[0]
    n = T * TOP_K
    n_tiles = n // TM_MOE + N_EXPERTS
    n_rows = n_tiles * TM_MOE
    top_v, top_i = lax.top_k(logits, TOP_K)
    gates = jax.nn.softmax(top_v, axis=-1)
    flat_e = top_i.reshape(n)
    onehot = (flat_e[:, None] == jnp.arange(N_EXPERTS, dtype=jnp.int32)[None, :]).astype(jnp.int32)
    csum = jnp.cumsum(onehot, axis=0)
    pos = jnp.sum(csum * onehot, axis=1) - 1
    counts = csum[-1]
    tiles_e = (counts + TM_MOE - 1) // TM_MOE
    tile_end = jnp.cumsum(tiles_e)
    tile_start = tile_end - tiles_e
    dest = (tile_start[flat_e] * TM_MOE + pos).astype(jnp.int32)
    row_tok = jnp.zeros((n_rows,), jnp.int32).at[dest].set(jnp.arange(n, dtype=jnp.int32) // TOP_K)
    n_used = tile_end[-1].astype(jnp.int32)
    tiles = jnp.arange(n_tiles, dtype=jnp.int32)
    t_eff = jnp.minimum(tiles, n_used - 1)
    tile_expert = jnp.minimum(jnp.searchsorted(tile_end, t_eff, side='right'),
                              N_EXPERTS - 1).astype(jnp.int32)
    rows_in_tile = jnp.clip(counts[tile_expert] - (tiles - tile_start[tile_expert]) * TM_MOE, 0, TM_MOE)
    tile_nsub = jnp.where(tiles < n_used, (rows_in_tile + SUB_MOE - 1) // SUB_MOE, 0).astype(jnp.int32)
    return gates, dest, row_tok, tile_expert, tile_nsub, n_used.reshape(1)


def _rope_tables(seq):
    inv = ROPE_THETA ** (-jnp.arange(0, HEAD_DIM, 2, dtype=f32) / HEAD_DIM)
    ang = jnp.arange(seq, dtype=f32)[:, None] * inv[None, :]
    ang = jnp.concatenate([ang, ang, ang, ang], -1)
    first_half = (jnp.arange(LANES) % HEAD_DIM) < HEAD_DIM // 2
    return jnp.cos(ang), jnp.where(first_half[None, :], -jnp.sin(ang), jnp.sin(ang))


def _trunk(x, p):
    B, S, _ = x.shape
    T = B * S
    x2d = x.reshape(T, D_MODEL)
    cos, sin_signed = _rope_tables(S)
    proj = _in_proj(x2d, p["w_in"], cos, sin_signed, p["bias_full"], S)
    o = _attention(proj.reshape(B, S, IN_COLS), p["lam"], p["subln_g"])
    x1, logits = _post(o.reshape(T, ATTN_WIDTH), proj, x2d, p["wa"], p["wc"], p["wo"], p["conv_w"],
                       p["ln1_g"], p["ln1_b"], p["wr_hi"], p["wr_lo"], p["br"], S)
    gates, dest, row_tok, tile_expert, tile_nsub, n_used = _route(logits)
    xs = _gather_rows(x1, row_tok)
    ys = _moe_experts(xs, p["w1"], p["b1"], p["w2"], p["b2"], tile_expert, tile_nsub, n_used)
    y = _combine(dest, ys, x1, gates, p["ln2_g"], p["ln2_b"])
    return y.reshape(B, S, D_MODEL)


def kernel(x_prompt, x_sample, w_in, b_gate, lambda_q1, lambda_k1, lambda_q2, lambda_k2, subln_g,
           conv_w, w_attn_out, w_conv_out, w_o, ln1_g, ln1_b, w_router, b_router, w1, b1, w2, b2,
           ln2_g, ln2_b):
    l = 0
    lam = (jnp.exp(jnp.sum(lambda_q1[l].astype(f32) * lambda_k1[l].astype(f32)))
           - jnp.exp(jnp.sum(lambda_q2[l].astype(f32) * lambda_k2[l].astype(f32)))
           + LAMBDA_INIT)
    wr = w_router[l]
    wr_hi = wr.astype(bf16)
    p = {
        "w_in": w_in[l].astype(bf16),
        "bias_full": jnp.concatenate([jnp.zeros((IN_COLS - GATE_COLS,), f32), b_gate[l]]).reshape(1, IN_COLS),
        "lam": lam.reshape(1, 1).astype(f32),
        "subln_g": subln_g[l].reshape(1, V_DIM),
        "wa": w_attn_out[l].astype(bf16),
        "wc": w_conv_out[l].astype(bf16),
        "wo": w_o[l].astype(bf16),
        "conv_w": conv_w[l],
        "ln1_g": ln1_g[l].reshape(1, D_MODEL),
        "ln1_b": ln1_b[l].reshape(1, D_MODEL),
        "wr_hi": wr_hi,
        "wr_lo": (wr - wr_hi.astype(f32)).astype(bf16),
        "br": b_router[l].reshape(1, N_EXPERTS),
        "w1": w1[l].astype(bf16),
        "b1": b1[l].reshape(N_EXPERTS, 1, 2 * D_FF),
        "w2": w2[l].astype(bf16),
        "b2": b2[l].reshape(N_EXPERTS, 1, D_MODEL),
        "ln2_g": ln2_g[l].reshape(1, D_MODEL),
        "ln2_b": ln2_b[l].reshape(1, D_MODEL),
    }
    return (_trunk(x_prompt, p), _trunk(x_sample, p))
```

```python
import functools

import numpy as np
import jax
import jax.numpy as jnp
from jax import lax
from jax.experimental import pallas as pl
from jax.experimental.pallas import tpu as pltpu

D_MODEL = 2048
N_HEADS = 8
HEAD_DIM = 64
V_DIM = 2 * HEAD_DIM
Q_COLS = N_HEADS * 2 * HEAD_DIM
ATTN_WIDTH = N_HEADS * V_DIM
ROPE_THETA = 10000.0
CONV_WIDTH = D_MODEL // 2
GATE_COLS = 2 * D_MODEL
IN_COLS = 2 * Q_COLS + ATTN_WIDTH + 3 * CONV_WIDTH + GATE_COLS
N_EXPERTS = 32
TOP_K = 4
D_FF = D_MODEL
SWIGLU_LIMIT = 7.0
SWIGLU_ALPHA = 1.702
LN_EPS = 1e-5
RMS_EPS = 1e-5
DEPTH = 1
DN_ALPHA = (2.0 * DEPTH) ** 0.25
LAMBDA_INIT = 0.8 - 0.6 * float(np.exp(-0.3 * 0))

LANES = 128
SUBLANES = 8
BF16_SUBLANES = 16
VMEM_LIMIT = 56 * 1024 * 1024
HALF = D_MODEL // 2
PACK_TILES = HALF // LANES
assert PACK_TILES == SUBLANES

PROJ_TN = 1024
COL_Q, COL_K, COL_V, COL_CB, COL_CC, COL_CX = 0, 1, 2, 3, 4, 5
COL_GATE0 = 6
N_PROJ_TILES = IN_COLS // PROJ_TN

TM_PROJ = 512
TQ = 256
TM_POST = 256
TM_MOE = 1024
SUB_MOE = 256
TF_MOE = 512
N_F = D_FF // TF_MOE
TC = 256
ISSUE_UNROLL = 8

f32 = jnp.float32
bf16 = jnp.bfloat16
u32 = jnp.uint32
HI_MASK = np.uint32(0xFFFF0000)


def _cparams(n_axes):
    return pltpu.CompilerParams(dimension_semantics=("arbitrary",) * n_axes,
                                vmem_limit_bytes=VMEM_LIMIT)


def _pack_words(v):
    hi = lax.bitcast_convert_type(v[:, :HALF].astype(bf16).astype(f32), u32)
    lo = lax.bitcast_convert_type(v[:, HALF:].astype(bf16).astype(f32), u32)
    return hi | (lo >> 16)


def _store_token_major(ref, row0, words):
    m = words.shape[0]
    for k in range(PACK_TILES):
        ref[pl.ds(row0 + k, m, stride=SUBLANES), :] = words[:, k * LANES:(k + 1) * LANES]


def _load_token_major(ref, row0, m, k):
    w = ref[pl.ds(row0 + k, m, stride=SUBLANES), :]
    return (lax.bitcast_convert_type(w & HI_MASK, f32),
            lax.bitcast_convert_type(w << 16, f32))


def _in_proj_kernel(x_ref, w_ref, cos_ref, sin_ref, bias_ref, o_ref, xb_ref):
    j = pl.program_id(1)

    @pl.when(j == 0)
    def _():
        xb_ref[...] = x_ref[...].astype(bf16)

    acc = jnp.dot(xb_ref[...], w_ref[...], preferred_element_type=f32)

    @pl.when(j <= COL_K)
    def _():
        scale = jnp.where(j == COL_Q, HEAD_DIM ** -0.5, 1.0).astype(f32)
        cos = cos_ref[...]
        sin = sin_ref[...]
        first_half = (lax.broadcasted_iota(jnp.int32, cos.shape, 1) % HEAD_DIM) < HEAD_DIM // 2
        for c in range(PROJ_TN // LANES):
            t = acc[:, c * LANES:(c + 1) * LANES]
            rot = jnp.where(first_half,
                            pltpu.roll(t, LANES - HEAD_DIM // 2, 1),
                            pltpu.roll(t, HEAD_DIM // 2, 1))
            o_ref[:, c * LANES:(c + 1) * LANES] = ((t * cos + rot * sin) * scale).astype(o_ref.dtype)

    @pl.when((j > COL_K) & (j < COL_GATE0))
    def _():
        o_ref[...] = acc.astype(o_ref.dtype)

    @pl.when(j >= COL_GATE0)
    def _():
        o_ref[...] = jax.nn.sigmoid(acc + bias_ref[...]).astype(o_ref.dtype)


def _in_proj(x2d, w_in_b, cos, sin_signed, bias_full, seq):
    T = x2d.shape[0]
    tm = TM_PROJ
    pos_tiles = seq // tm
    return pl.pallas_call(
        _in_proj_kernel,
        grid=(T // tm, N_PROJ_TILES),
        in_specs=[
            pl.BlockSpec((tm, D_MODEL), lambda i, j: (i, 0)),
            pl.BlockSpec((D_MODEL, PROJ_TN), lambda i, j: (0, j)),
            pl.BlockSpec((tm, LANES), lambda i, j: (i % pos_tiles, 0)),
            pl.BlockSpec((tm, LANES), lambda i, j: (i % pos_tiles, 0)),
            pl.BlockSpec((1, PROJ_TN), lambda i, j: (0, j)),
        ],
        out_specs=pl.BlockSpec((tm, PROJ_TN), lambda i, j: (i, j)),
        out_shape=jax.ShapeDtypeStruct((T, IN_COLS), bf16),
        scratch_shapes=[pltpu.VMEM((tm, D_MODEL), bf16)],
        compiler_params=_cparams(2),
        name="in_proj",
    )(x2d, w_in_b, cos, sin_signed, bias_full)


def _attn_kernel(lam_ref, q_ref, k_ref, v_ref, g_ref, o_ref, kt_ref):
    i = pl.program_id(2)

    @pl.when(i == 0)
    def _():
        kt_ref[...] = k_ref[...].astype(f32).T.astype(bf16)

    lam = lam_ref[0, 0]
    q = q_ref[...]
    first = lax.broadcasted_iota(jnp.int32, q.shape, 1) < HEAD_DIM
    zero = jnp.zeros_like(q)
    kt = kt_ref[...]
    s1 = jnp.dot(jnp.where(first, q, zero), kt, preferred_element_type=f32)
    s2 = jnp.dot(jnp.where(first, zero, q), kt, preferred_element_type=f32)
    e1 = jnp.exp(s1 - jnp.max(s1, axis=-1, keepdims=True))
    e2 = jnp.exp(s2 - jnp.max(s2, axis=-1, keepdims=True))
    a1 = 1.0 / jnp.sum(e1, axis=-1, keepdims=True)
    a2 = lam / jnp.sum(e2, axis=-1, keepdims=True)
    pd = (e1 * a1 - e2 * a2).astype(bf16)
    o = jnp.dot(pd, v_ref[...], preferred_element_type=f32)
    o = o * lax.rsqrt(jnp.mean(o * o, axis=-1, keepdims=True) + RMS_EPS)
    o_ref[...] = (o * g_ref[...] * (1.0 - LAMBDA_INIT)).astype(o_ref.dtype)


def _attention(proj3, lam, subln_g):
    B, S, _ = proj3.shape
    hb = PROJ_TN // LANES
    return pl.pallas_call(
        _attn_kernel,
        grid=(B, N_HEADS, S // TQ),
        in_specs=[
            pl.BlockSpec(memory_space=pltpu.SMEM),
            pl.BlockSpec((None, TQ, LANES), lambda b, h, i: (b, i, COL_Q * hb + h)),
            pl.BlockSpec((None, S, LANES), lambda b, h, i: (b, 0, COL_K * hb + h)),
            pl.BlockSpec((None, S, LANES), lambda b, h, i: (b, 0, COL_V * hb + h)),
            pl.BlockSpec((1, LANES), lambda b, h, i: (0, 0)),
        ],
        out_specs=pl.BlockSpec((None, TQ, LANES), lambda b, h, i: (b, i, h)),
        out_shape=jax.ShapeDtypeStruct((B, S, ATTN_WIDTH), bf16),
        scratch_shapes=[pltpu.VMEM((LANES, S), bf16)],
        compiler_params=_cparams(3),
        name="diff_attn",
    )(lam, proj3, proj3, proj3, subln_g)


def _layer_norm(r, g, b):
    mu = jnp.mean(r, axis=-1, keepdims=True)
    d = r - mu
    var = jnp.mean(d * d, axis=-1, keepdims=True)
    return d * lax.rsqrt(var + LN_EPS) * g + b


def _post_kernel(seq, o_ref, cb_ref, cc_ref, cx_ref, ccp_ref, cxp_ref, ccn_ref, cxn_ref,
                 ga_ref, gc_ref, x_ref, wa_ref, wc_ref, wo_ref, cw_ref, g1_ref, b1_ref,
                 wrh_ref, wrl_ref, br_ref, x1_ref, x1p_ref, lg_ref):
    i = pl.program_id(0)
    tm = x_ref.shape[0]
    h = cc_ref[...].astype(f32) * cx_ref[...].astype(f32)
    last = BF16_SUBLANES - 1
    h_prev = ccp_ref[last:last + 1, :].astype(f32) * cxp_ref[last:last + 1, :].astype(f32)
    h_next = ccn_ref[0:1, :].astype(f32) * cxn_ref[0:1, :].astype(f32)
    h_prev = jnp.where((i * tm) % seq == 0, 0.0, h_prev)
    h_next = jnp.where(((i + 1) * tm) % seq == 0, 0.0, h_next)
    row = lax.broadcasted_iota(jnp.int32, h.shape, 0)
    h_m1 = jnp.where(row == 0, h_prev, pltpu.roll(h, 1, 0))
    h_p1 = jnp.where(row == tm - 1, h_next, pltpu.roll(h, tm - 1, 0))
    cw = cw_ref[...]
    conv = h_m1 * cw[0:1, :] + h * cw[1:2, :] + h_p1 * cw[2:3, :]
    yc = (cb_ref[...].astype(f32) * conv).astype(bf16)
    branch_a = jnp.dot(o_ref[...], wa_ref[...], preferred_element_type=f32)
    branch_c = jnp.dot(yc, wc_ref[...], preferred_element_type=f32)
    mix_in = ga_ref[...].astype(f32) * branch_a + gc_ref[...].astype(f32) * branch_c
    mix = jnp.dot(mix_in.astype(bf16), wo_ref[...], preferred_element_type=f32)
    x1 = _layer_norm(DN_ALPHA * x_ref[...] + mix, g1_ref[...], b1_ref[...])
    x1_ref[...] = x1
    _store_token_major(x1p_ref, 0, _pack_words(x1))
    hi = x1.astype(bf16)
    lo = (x1 - hi.astype(f32)).astype(bf16)
    lg = (jnp.dot(hi, wrh_ref[...], preferred_element_type=f32)
          + jnp.dot(lo, wrh_ref[...], preferred_element_type=f32)
          + jnp.dot(hi, wrl_ref[...], preferred_element_type=f32))
    lg_ref[...] = lg + br_ref[...]


def _post(o2d, proj, x2d, wa, wc, wo, conv_w, g1, b1, wr_hi, wr_lo, br, seq):
    T = x2d.shape[0]
    tm = TM_POST
    halo = tm // BF16_SUBLANES
    n_halo = T // BF16_SUBLANES
    gate_blk = COL_GATE0 * PROJ_TN // D_MODEL

    def const(shape):
        return pl.BlockSpec(shape, lambda i: (0,) * len(shape))

    def prev_spec(col):
        return pl.BlockSpec((BF16_SUBLANES, PROJ_TN), lambda i: (jnp.maximum(i * halo - 1, 0), col))

    def next_spec(col):
        return pl.BlockSpec((BF16_SUBLANES, PROJ_TN),
                            lambda i: (jnp.minimum((i + 1) * halo, n_halo - 1), col))

    return pl.pallas_call(
        functools.partial(_post_kernel, seq),
        grid=(T // tm,),
        in_specs=[
            pl.BlockSpec((tm, ATTN_WIDTH), lambda i: (i, 0)),
            pl.BlockSpec((tm, PROJ_TN), lambda i: (i, COL_CB)),
            pl.BlockSpec((tm, PROJ_TN), lambda i: (i, COL_CC)),
            pl.BlockSpec((tm, PROJ_TN), lambda i: (i, COL_CX)),
            prev_spec(COL_CC), prev_spec(COL_CX), next_spec(COL_CC), next_spec(COL_CX),
            pl.BlockSpec((tm, D_MODEL), lambda i: (i, gate_blk)),
            pl.BlockSpec((tm, D_MODEL), lambda i: (i, gate_blk + 1)),
            pl.BlockSpec((tm, D_MODEL), lambda i: (i, 0)),
            const((ATTN_WIDTH, D_MODEL)), const((CONV_WIDTH, D_MODEL)), const((D_MODEL, D_MODEL)),
            const((3, CONV_WIDTH)), const((1, D_MODEL)), const((1, D_MODEL)),
            const((D_MODEL, N_EXPERTS)), const((D_MODEL, N_EXPERTS)), const((1, N_EXPERTS)),
        ],
        out_specs=[pl.BlockSpec((tm, D_MODEL), lambda i: (i, 0)),
                   pl.BlockSpec((tm * SUBLANES, LANES), lambda i: (i, 0)),
                   pl.BlockSpec((tm, N_EXPERTS), lambda i: (i, 0))],
        out_shape=[jax.ShapeDtypeStruct((T, D_MODEL), f32),
                   jax.ShapeDtypeStruct((T * SUBLANES, LANES), u32),
                   jax.ShapeDtypeStruct((T, N_EXPERTS), f32)],
        compiler_params=_cparams(1),
        name="post_attn",
    )(o2d, proj, proj, proj, proj, proj, proj, proj, proj, proj, x2d,
      wa, wc, wo, conv_w, g1, b1, wr_hi, wr_lo, br)


def _token_copy(src_ref, src_tok, dst_ref, dst_tok, sem):
    return pltpu.make_async_copy(
        src_ref.at[pl.ds(pl.multiple_of(src_tok * SUBLANES, SUBLANES), SUBLANES), :],
        dst_ref.at[pl.ds(pl.multiple_of(dst_tok * SUBLANES, SUBLANES), SUBLANES), :],
        sem)


def _moe_kernel(te_ref, ns_ref, nu_ref, idxc_ref, idxn_ref, x_ref, w1g_ref, w1u_ref, b1g_ref,
                b1u_ref, w2_ref, b2_ref, o_ref, xg_ref, xb_ref, acc_ref, sem):
    t = pl.program_id(0)
    c = pl.program_id(1)
    n_used = nu_ref[0]
    slot = t % 2
    nsub = ns_ref[t]

    def issue(idx_ref, sl, lo, hi):
        def group(g, carry):
            for u in range(ISSUE_UNROLL):
                r = g * ISSUE_UNROLL + u
                _token_copy(x_ref, idx_ref[r], xg_ref, sl * TM_MOE + r, sem.at[sl]).start()
            return carry
        lax.fori_loop(lo // ISSUE_UNROLL, hi // ISSUE_UNROLL, group, 0)

    @pl.when((t == 0) & (c == 0))
    def _():
        issue(idxc_ref, 0, 0, ns_ref[0] * SUB_MOE)

    @pl.when(t + 1 < n_used)
    def _():
        n_next = ns_ref[t + 1] * SUB_MOE
        quarter = TM_MOE // N_F
        issue(idxn_ref, 1 - slot, jnp.minimum(c * quarter, n_next),
              jnp.minimum((c + 1) * quarter, n_next))

    @pl.when(c == 0)
    def _():
        def wait_sub(s, carry):
            pltpu.make_async_copy(x_ref.at[pl.ds(0, SUB_MOE * SUBLANES), :],
                                  xg_ref.at[pl.ds(0, SUB_MOE * SUBLANES), :], sem.at[slot]).wait()
            return carry

        lax.fori_loop(0, nsub, wait_sub, 0)

        def unpack(s, carry):
            r0 = pl.multiple_of(s * SUB_MOE, SUB_MOE)
            row0 = (slot * TM_MOE + r0) * SUBLANES
            for k in range(PACK_TILES):
                hi, lo = _load_token_major(xg_ref, row0, SUB_MOE, k)
                xb_ref[pl.ds(r0, SUB_MOE), k * LANES:(k + 1) * LANES] = hi.astype(bf16)
                xb_ref[pl.ds(r0, SUB_MOE), HALF + k * LANES:HALF + (k + 1) * LANES] = lo.astype(bf16)
            return carry

        lax.fori_loop(0, nsub, unpack, 0)
        acc_ref[...] = jnp.broadcast_to(b2_ref[...], acc_ref.shape)

    def body(s, carry):
        r0 = pl.multiple_of(s * SUB_MOE, SUB_MOE)
        xb = xb_ref[pl.ds(r0, SUB_MOE), :]
        hg = jnp.dot(xb, w1g_ref[...], preferred_element_type=f32) + b1g_ref[...]
        hu = jnp.dot(xb, w1u_ref[...], preferred_element_type=f32) + b1u_ref[...]
        gate = jnp.minimum(hg, SWIGLU_LIMIT)
        up = jnp.clip(hu, -SWIGLU_LIMIT, SWIGLU_LIMIT)
        act = gate * jax.nn.sigmoid(SWIGLU_ALPHA * gate)
        a = ((up + 1.0) * act).astype(bf16)
        acc_ref[pl.ds(r0, SUB_MOE), :] += jnp.dot(a, w2_ref[...], preferred_element_type=f32)
        return carry

    lax.fori_loop(0, nsub, body, 0)

    @pl.when(c == N_F - 1)
    def _():
        def emit(s, carry):
            r0 = pl.multiple_of(s * SUB_MOE, SUB_MOE)
            _store_token_major(o_ref, r0 * SUBLANES, _pack_words(acc_ref[pl.ds(r0, SUB_MOE), :]))
            return carry

        lax.fori_loop(0, nsub, emit, 0)

        def clear(s, carry):
            r0 = pl.multiple_of(s * SUB_MOE * SUBLANES, SUB_MOE * SUBLANES)
            o_ref[pl.ds(r0, SUB_MOE * SUBLANES), :] = jnp.zeros((SUB_MOE * SUBLANES, LANES), u32)
            return carry

        lax.fori_loop(nsub, TM_MOE // SUB_MOE, clear, 0)


def _moe_experts(x1p, row_tok, w1b, b1, w2b, b2, tile_expert, tile_nsub, n_used):
    n_rows = row_tok.shape[0]
    n_tiles = n_rows // TM_MOE

    def f_idx(t, c, nu):
        return jnp.where(t < nu[0], c, N_F - 1)

    grid_spec = pltpu.PrefetchScalarGridSpec(
        num_scalar_prefetch=3,
        grid=(n_tiles, N_F),
        in_specs=[
            pl.BlockSpec((TM_MOE,), lambda t, c, te, ns, nu: (jnp.minimum(t, nu[0] - 1),),
                         memory_space=pltpu.SMEM),
            pl.BlockSpec((TM_MOE,), lambda t, c, te, ns, nu: (jnp.minimum(t + 1, nu[0] - 1),),
                         memory_space=pltpu.SMEM),
            pl.BlockSpec(memory_space=pl.ANY),
            pl.BlockSpec((None, D_MODEL, TF_MOE), lambda t, c, te, ns, nu: (te[t], 0, f_idx(t, c, nu))),
            pl.BlockSpec((None, D_MODEL, TF_MOE),
                         lambda t, c, te, ns, nu: (te[t], 0, N_F + f_idx(t, c, nu))),
            pl.BlockSpec((None, 1, TF_MOE), lambda t, c, te, ns, nu: (te[t], 0, f_idx(t, c, nu))),
            pl.BlockSpec((None, 1, TF_MOE), lambda t, c, te, ns, nu: (te[t], 0, N_F + f_idx(t, c, nu))),
            pl.BlockSpec((None, TF_MOE, D_MODEL), lambda t, c, te, ns, nu: (te[t], f_idx(t, c, nu), 0)),
            pl.BlockSpec((None, 1, D_MODEL), lambda t, c, te, ns, nu: (te[t], 0, 0)),
        ],
        out_specs=pl.BlockSpec((TM_MOE * SUBLANES, LANES), lambda t, c, te, ns, nu: (t, 0)),
        scratch_shapes=[
            pltpu.VMEM((2 * TM_MOE * SUBLANES, LANES), u32),
            pltpu.VMEM((TM_MOE, D_MODEL), bf16),
            pltpu.VMEM((TM_MOE, D_MODEL), f32),
            pltpu.SemaphoreType.DMA((2,)),
        ],
    )
    return pl.pallas_call(
        _moe_kernel,
        grid_spec=grid_spec,
        out_shape=jax.ShapeDtypeStruct((n_rows * SUBLANES, LANES), u32),
        compiler_params=_cparams(2),
        name="moe_experts",
    )(tile_expert, tile_nsub, n_used, row_tok, row_tok, x1p, w1b, w1b, b1, b1, w2b, b2)


def _combine_kernel(dc_ref, dn_ref, ys_ref, x1_ref, gates_ref, g_ref, b_ref, o_ref, buf_ref, sem):
    i = pl.program_id(0)
    slot = i % 2

    def issue(d_ref, sl):
        def one(r, carry):
            for k in range(TOP_K):
                _token_copy(ys_ref, d_ref[r * TOP_K + k], buf_ref, (sl * TOP_K + k) * TC + r,
                            sem.at[sl]).start()
            return carry
        lax.fori_loop(0, TC, one, 0)

    @pl.when(i == 0)
    def _():
        issue(dc_ref, 0)

    @pl.when(i + 1 < pl.num_programs(0))
    def _():
        issue(dn_ref, 1 - slot)

    for k in range(TOP_K):
        pltpu.make_async_copy(ys_ref.at[pl.ds(0, TC * SUBLANES), :],
                              buf_ref.at[pl.ds(0, TC * SUBLANES), :], sem.at[slot]).wait()
    gates = gates_ref[...]
    gk = [jnp.broadcast_to(gates[:, k:k + 1], (TC, LANES)) for k in range(TOP_K)]
    for j in range(PACK_TILES):
        y_hi = jnp.zeros((TC, LANES), f32)
        y_lo = jnp.zeros((TC, LANES), f32)
        for k in range(TOP_K):
            hi, lo = _load_token_major(buf_ref, (slot * TOP_K + k) * TC * SUBLANES, TC, j)
            y_hi = y_hi + gk[k] * hi
            y_lo = y_lo + gk[k] * lo
        c0 = j * LANES
        o_ref[:, c0:c0 + LANES] = DN_ALPHA * x1_ref[:, c0:c0 + LANES] + y_hi
        o_ref[:, HALF + c0:HALF + c0 + LANES] = DN_ALPHA * x1_ref[:, HALF + c0:HALF + c0 + LANES] + y_lo
    o_ref[...] = _layer_norm(o_ref[...], g_ref[...], b_ref[...])


def _combine(dest, ysp, x1, gates, g2, b2):
    T = x1.shape[0]
    n = T // TC
    return pl.pallas_call(
        _combine_kernel,
        grid=(n,),
        in_specs=[
            pl.BlockSpec((TC * TOP_K,), lambda i: (i,), memory_space=pltpu.SMEM),
            pl.BlockSpec((TC * TOP_K,), lambda i: (jnp.minimum(i + 1, n - 1),), memory_space=pltpu.SMEM),
            pl.BlockSpec(memory_space=pl.ANY),
            pl.BlockSpec((TC, D_MODEL), lambda i: (i, 0)),
            pl.BlockSpec((TC, TOP_K), lambda i: (i, 0)),
            pl.BlockSpec((1, D_MODEL), lambda i: (0, 0)),
            pl.BlockSpec((1, D_MODEL), lambda i: (0, 0)),
        ],
        out_specs=pl.BlockSpec((TC, D_MODEL), lambda i: (i, 0)),
        out_shape=jax.ShapeDtypeStruct((T, D_MODEL), f32),
        scratch_shapes=[pltpu.VMEM((2 * TOP_K * TC * SUBLANES, LANES), u32),
                        pltpu.SemaphoreType.DMA((2,))],
        compiler_params=_cparams(1),
        name="moe_combine",
    )(dest, dest, ysp, x1, gates, g2, b2)


def _route(logits):
    T = logits.shape[0]
    n = T * TOP_K
    n_tiles = n // TM_MOE + N_EXPERTS
    n_rows = n_tiles * TM_MOE
    top_v, top_i = lax.top_k(logits, TOP_K)
    gates = jax.nn.softmax(top_v, axis=-1)
    flat_e = top_i.reshape(n)
    onehot = (flat_e[:, None] == jnp.arange(N_EXPERTS, dtype=jnp.int32)[None, :]).astype(jnp.int32)
    csum = jnp.cumsum(onehot, axis=0)
    pos = jnp.sum(csum * onehot, axis=1) - 1
    counts = csum[-1]
    tiles_e = (counts + TM_MOE - 1) // TM_MOE
    tile_end = jnp.cumsum(tiles_e)
    tile_start = tile_end - tiles_e
    dest = (tile_start[flat_e] * TM_MOE + pos).astype(jnp.int32)
    row_tok = jnp.zeros((n_rows,), jnp.int32).at[dest].set(jnp.arange(n, dtype=jnp.int32) // TOP_K)
    n_used = tile_end[-1].astype(jnp.int32)
    tiles = jnp.arange(n_tiles, dtype=jnp.int32)
    t_eff = jnp.minimum(tiles, n_used - 1)
    tile_expert = jnp.minimum(jnp.searchsorted(tile_end, t_eff, side='right'),
                              N_EXPERTS - 1).astype(jnp.int32)
    rows_in_tile = jnp.clip(counts[tile_expert] - (tiles - tile_start[tile_expert]) * TM_MOE, 0, TM_MOE)
    tile_nsub = jnp.where(tiles < n_used, (rows_in_tile + SUB_MOE - 1) // SUB_MOE, 0).astype(jnp.int32)
    return gates, dest, row_tok, tile_expert, tile_nsub, n_used.reshape(1)


def _rope_tables(seq):
    inv = ROPE_THETA ** (-jnp.arange(0, HEAD_DIM, 2, dtype=f32) / HEAD_DIM)
    ang = jnp.arange(seq, dtype=f32)[:, None] * inv[None, :]
    ang = jnp.concatenate([ang, ang, ang, ang], -1)
    first_half = (jnp.arange(LANES) % HEAD_DIM) < HEAD_DIM // 2
    return jnp.cos(ang), jnp.where(first_half[None, :], -jnp.sin(ang), jnp.sin(ang))


def _trunk(x, p):
    B, S, _ = x.shape
    T = B * S
    x2d = x.reshape(T, D_MODEL)
    cos, sin_signed = _rope_tables(S)
    proj = _in_proj(x2d, p["w_in"], cos, sin_signed, p["bias_full"], S)
    o = _attention(proj.reshape(B, S, IN_COLS), p["lam"], p["subln_g"])
    x1, x1p, logits = _post(o.reshape(T, ATTN_WIDTH), proj, x2d, p["wa"], p["wc"], p["wo"],
                            p["conv_w"], p["ln1_g"], p["ln1_b"], p["wr_hi"], p["wr_lo"], p["br"], S)
    gates, dest, row_tok, tile_expert, tile_nsub, n_used = _route(logits)
    ysp = _moe_experts(x1p, row_tok, p["w1"], p["b1"], p["w2"], p["b2"], tile_expert, tile_nsub, n_used)
    y = _combine(dest, ysp, x1, gates, p["ln2_g"], p["ln2_b"])
    return y.reshape(B, S, D_MODEL)


def kernel(x_prompt, x_sample, w_in, b_gate, lambda_q1, lambda_k1, lambda_q2, lambda_k2, subln_g,
           conv_w, w_attn_out, w_conv_out, w_o, ln1_g, ln1_b, w_router, b_router, w1, b1, w2, b2,
           ln2_g, ln2_b):
    l = 0
    lam = (jnp.exp(jnp.sum(lambda_q1[l].astype(f32) * lambda_k1[l].astype(f32)))
           - jnp.exp(jnp.sum(lambda_q2[l].astype(f32) * lambda_k2[l].astype(f32)))
           + LAMBDA_INIT)
    wr = w_router[l]
    wr_hi = wr.astype(bf16)
    p = {
        "w_in": w_in[l].astype(bf16),
        "bias_full": jnp.concatenate([jnp.zeros((IN_COLS - GATE_COLS,), f32), b_gate[l]]).reshape(1, IN_COLS),
        "lam": lam.reshape(1, 1).astype(f32),
        "subln_g": subln_g[l].reshape(1, V_DIM),
        "wa": w_attn_out[l].astype(bf16),
        "wc": w_conv_out[l].astype(bf16),
        "wo": w_o[l].astype(bf16),
        "conv_w": conv_w[l],
        "ln1_g": ln1_g[l].reshape(1, D_MODEL),
        "ln1_b": ln1_b[l].reshape(1, D_MODEL),
        "wr_hi": wr_hi,
        "wr_lo": (wr - wr_hi.astype(f32)).astype(bf16),
        "br": b_router[l].reshape(1, N_EXPERTS),
        "w1": w1[l].astype(bf16),
        "b1": b1[l].reshape(N_EXPERTS, 1, 2 * D_FF),
        "w2": w2[l].astype(bf16),
        "b2": b2[l].reshape(N_EXPERTS, 1, D_MODEL),
        "ln2_g": ln2_g[l].reshape(1, D_MODEL),
        "ln2_b": ln2_b[l].reshape(1, D_MODEL),
    }
    return (_trunk(x_prompt, p), _trunk(x_sample, p))
```
